```python
import jax, jax.numpy as jnp
from jax import lax
import numpy as np

D_MODEL = 2048
BATCH = 1
SEQ = 8192
DEPTH = 2
DEC_BATCH = 32
DEC_SEQ = 64
PAST_LEN = 4096

CHUNK = 64
Q_BLOCK = 128
N_HEADS = 16
QK_NOPE = 128
QK_ROPE = 64
QK_HEAD = QK_NOPE + QK_ROPE
V_HEAD = 128
Q_LORA = 512
KV_LORA = 512
MLA_WIDTH = N_HEADS * V_HEAD
MLA_SCALE = QK_HEAD ** -0.5
ROPE_THETA = 10000.0
CONV_WIDTH = 1024
CONV_K = 31
CONV_STATE = CONV_K - 1
N_MEM = 256
X_HEADS = 4
X_HEAD = 256
X_WIDTH = X_HEADS * X_HEAD
X_SCALE = X_HEAD ** -0.5
N_BRANCH = 3
EPS = 1e-6
IN_SIZES = (Q_LORA, KV_LORA, QK_ROPE, MLA_WIDTH, 2 * CONV_WIDTH, CONV_WIDTH, X_WIDTH, X_WIDTH, N_BRANCH * D_MODEL)
IN_COLS = Q_LORA + KV_LORA + QK_ROPE + MLA_WIDTH + 2 * CONV_WIDTH + CONV_WIDTH + X_WIDTH + X_WIDTH + N_BRANCH * D_MODEL

kernel_name = "hybrid_mla_conformer_memory_stream_step"


def rmsnorm(x, g):
    xf = x.astype(jnp.float32)
    y = xf * lax.rsqrt(jnp.mean(xf * xf, axis=-1, keepdims=True) + EPS)
    return (y * g.astype(jnp.float32)).astype(x.dtype)


def layernorm(x, g, b):
    xf = x.astype(jnp.float32)
    mu = jnp.mean(xf, axis=-1, keepdims=True)
    xc = xf - mu
    var = jnp.mean(xc * xc, axis=-1, keepdims=True)
    return (xc * lax.rsqrt(var + EPS) * g.astype(jnp.float32) + b.astype(jnp.float32)).astype(x.dtype)


def rope(x, pos):
    half = QK_ROPE // 2
    inv = jnp.power(ROPE_THETA, -jnp.arange(half, dtype=jnp.float32) / half)
    ang = pos.astype(jnp.float32)[:, None] * inv[None, :]
    cos = jnp.cos(ang)[:, None, :]
    sin = jnp.sin(ang)[:, None, :]
    xf = x.astype(jnp.float32)
    x1, x2 = xf[..., :half], xf[..., half:]
    return jnp.concatenate([x1 * cos - x2 * sin, x2 * cos + x1 * sin], axis=-1).astype(x.dtype)


def attend(q, k, v, scale, visible=None):
    s = jnp.einsum('bqhd,bkhd->bhqk', q, k).astype(jnp.float32) * scale
    if visible is not None:
        s = jnp.where(visible, s, -jnp.inf)
    p = jax.nn.softmax(s, axis=-1)
    return jnp.einsum('bhqk,bkhd->bqhd', p.astype(v.dtype), v)


def chunk_causal_attention(q, k, v, scale):
    B, S, H, D = q.shape
    nb = S // Q_BLOCK
    qb = jnp.moveaxis(q.reshape(B, nb, Q_BLOCK, H, D), 1, 0)
    k_pos = jnp.arange(k.shape[1])

    def block(args):
        q_i, i = args
        q_pos = i * Q_BLOCK + jnp.arange(Q_BLOCK)
        visible = k_pos[None, :] < ((q_pos // CHUNK + 1) * CHUNK)[:, None]
        return attend(q_i, k, v, scale, visible)

    o = lax.map(block, (qb, jnp.arange(nb)))
    return jnp.moveaxis(o, 0, 1).reshape(B, S, H, v.shape[-1])


def mixer_projections(x, lp):
    h = rmsnorm(x, lp['g_pre'])
    proj = h @ lp['w_in']
    outs = []
    off = 0
    for size in IN_SIZES:
        outs.append(proj[..., off:off + size])
        off += size
    return outs


def mla_queries(q_lat, pos, lp):
    B, S = q_lat.shape[:2]
    q = (rmsnorm(q_lat, lp['g_qlat']) @ lp['w_uq']).reshape(B, S, N_HEADS, QK_HEAD)
    q = rmsnorm(q, lp['g_qnorm'])
    return jnp.concatenate([q[..., :QK_NOPE], rope(q[..., QK_NOPE:], pos)], axis=-1)


def mla_keys(c_kv, k_rope, pos, lp):
    B, T = c_kv.shape[:2]
    kv = (rmsnorm(c_kv, lp['g_kvlat']) @ lp['w_ukv']).reshape(B, T, N_HEADS, QK_NOPE + V_HEAD)
    k_pe = jnp.broadcast_to(k_rope[:, :, None, :], (B, T, N_HEADS, QK_ROPE))
    k = rmsnorm(jnp.concatenate([kv[..., :QK_NOPE], k_pe], axis=-1), lp['g_knorm'])
    k = jnp.concatenate([k[..., :QK_NOPE], rope(k[..., QK_NOPE:], pos)], axis=-1)
    return k, kv[..., QK_NOPE:]


def conv_module(u2, conv_state, lp):
    u = u2[..., :CONV_WIDTH] * jax.nn.sigmoid(u2[..., CONV_WIDTH:])
    u_pad = jnp.concatenate([conv_state, u], axis=1)
    y = lax.conv_general_dilated(u_pad, lp['conv_w'][:, None, :], window_strides=(1,), padding='VALID',
                                 dimension_numbers=('NWC', 'WIO', 'NWC'),
                                 feature_group_count=CONV_WIDTH) + lp['conv_b']
    y = jax.nn.silu(layernorm(y, lp['ln_g'], lp['ln_b']))
    y = y @ lp['w_pw2'] + lp['b_pw2']
    return y, u_pad[:, -CONV_STATE:]


def memory_kv(mem, lp):
    B, M = mem.shape[:2]
    kv = (rmsnorm(mem, lp['g_mem']) @ lp['w_mem_kv']).reshape(B, M, 2, X_HEADS, X_HEAD)
    return rmsnorm(kv[:, :, 0], lp['g_xk']), kv[:, :, 1]


def cross_attend(xq, mk, mv, lp):
    B, S = xq.shape[:2]
    q = rmsnorm(xq.reshape(B, S, X_HEADS, X_HEAD), lp['g_xq'])
    return attend(q, mk, mv, X_SCALE).reshape(B, S, X_WIDTH)


def merge_branches(x, o_mla, z_mla, o_conv, z_conv, o_x, z_x, g_merge, lp):
    B, S = x.shape[:2]
    p_mla = (o_mla * jax.nn.silu(z_mla)) @ lp['w_p_mla']
    p_conv = (o_conv * jax.nn.silu(z_conv)) @ lp['w_p_conv']
    p_x = (o_x * jax.nn.silu(z_x)) @ lp['w_p_x']
    gates = jax.nn.sigmoid(g_merge + lp['b_merge']).reshape(B, S, N_BRANCH, D_MODEL)
    merged = gates[:, :, 0] * p_mla + gates[:, :, 1] * p_conv + gates[:, :, 2] * p_x
    return x + merged @ lp['w_out']


def prompt_layer(x, mem, lp):
    B, S = x.shape[:2]
    q_lat, c_kv, k_rope, z_mla, u2, z_conv, xq, z_x, g_merge = mixer_projections(x, lp)
    pos = jnp.arange(S)
    q = mla_queries(q_lat, pos, lp)
    k, v = mla_keys(c_kv, k_rope, pos, lp)
    o_mla = chunk_causal_attention(q, k, v, MLA_SCALE).reshape(B, S, MLA_WIDTH)
    o_conv, conv_new = conv_module(u2, jnp.zeros((B, CONV_STATE, CONV_WIDTH), u2.dtype), lp)
    mk, mv = memory_kv(mem, lp)
    o_x = cross_attend(xq, mk, mv, lp)
    y = merge_branches(x, o_mla, z_mla, o_conv, z_conv, o_x, z_x, g_merge, lp)
    return y, c_kv, k_rope, mk, mv, conv_new


def sample_layer(x, ckv_past, krope_past, mk, mv, conv_state, lp):
    B, S = x.shape[:2]
    t_past = ckv_past.shape[1]
    q_lat, c_kv, k_rope, z_mla, u2, z_conv, xq, z_x, g_merge = mixer_projections(x, lp)
    q = mla_queries(q_lat, t_past + jnp.arange(S), lp)
    ckv_all = jnp.concatenate([ckv_past, c_kv], axis=1)
    kr_all = jnp.concatenate([krope_past, k_rope], axis=1)
    k_pos = jnp.arange(t_past + S)

    def stream(args):
        q_b, ckv_b, kr_b = args
        k, v = mla_keys(ckv_b[None], kr_b[None], k_pos, lp)
        return attend(q_b[None], k, v, MLA_SCALE)[0]

    o_mla = lax.map(stream, (q, ckv_all, kr_all)).reshape(B, S, MLA_WIDTH)
    o_conv, conv_new = conv_module(u2, conv_state, lp)
    o_x = cross_attend(xq, mk, mv, lp)
    y = merge_branches(x, o_mla, z_mla, o_conv, z_conv, o_x, z_x, g_merge, lp)
    return y, c_kv, k_rope, conv_new


def setup_inputs(seed: int = 0) -> dict:
    key = jax.random.key(seed)
    ks = iter(jax.random.split(key, 32))
    nrm = lambda shape, scale: jax.random.normal(next(ks), shape, jnp.float32) * scale
    gain = lambda shape: 1.0 + 0.02 * jax.random.normal(next(ks), shape, jnp.float32)
    return {
        "x_prompt": nrm((BATCH, SEQ, D_MODEL), 1.0),
        "x_sample": nrm((DEC_BATCH, DEC_SEQ, D_MODEL), 1.0),
        "mem_prompt": nrm((BATCH, N_MEM, D_MODEL), 1.0),
        "cache_ckv": nrm((DEPTH, DEC_BATCH, PAST_LEN, KV_LORA), 1.0),
        "cache_krope": nrm((DEPTH, DEC_BATCH, PAST_LEN, QK_ROPE), 1.0),
        "cache_mem_k": nrm((DEPTH, DEC_BATCH, N_MEM, X_HEADS, X_HEAD), 1.0),
        "cache_mem_v": nrm((DEPTH, DEC_BATCH, N_MEM, X_HEADS, X_HEAD), 1.0),
        "state_conv": nrm((DEPTH, DEC_BATCH, CONV_STATE, CONV_WIDTH), 0.5),
        "g_pre": gain((DEPTH, D_MODEL)),
        "w_in": nrm((DEPTH, D_MODEL, IN_COLS), D_MODEL ** -0.5),
        "g_qlat": gain((DEPTH, Q_LORA)),
        "w_uq": nrm((DEPTH, Q_LORA, N_HEADS * QK_HEAD), Q_LORA ** -0.5),
        "g_kvlat": gain((DEPTH, KV_LORA)),
        "w_ukv": nrm((DEPTH, KV_LORA, N_HEADS * (QK_NOPE + V_HEAD)), KV_LORA ** -0.5),
        "g_qnorm": gain((DEPTH, QK_HEAD)),
        "g_knorm": gain((DEPTH, QK_HEAD)),
        "conv_w": nrm((DEPTH, CONV_K, CONV_WIDTH), CONV_K ** -0.5),
        "conv_b": nrm((DEPTH, CONV_WIDTH), 0.02),
        "ln_g": gain((DEPTH, CONV_WIDTH)),
        "ln_b": nrm((DEPTH, CONV_WIDTH), 0.02),
        "w_pw2": nrm((DEPTH, CONV_WIDTH, CONV_WIDTH), CONV_WIDTH ** -0.5),
        "b_pw2": nrm((DEPTH, CONV_WIDTH), 0.02),
        "g_mem": gain((DEPTH, D_MODEL)),
        "w_mem_kv": nrm((DEPTH, D_MODEL, 2 * X_WIDTH), D_MODEL ** -0.5),
        "g_xq": gain((DEPTH, X_HEAD)),
        "g_xk": gain((DEPTH, X_HEAD)),
        "w_p_mla": nrm((DEPTH, MLA_WIDTH, D_MODEL), MLA_WIDTH ** -0.5),
        "w_p_conv": nrm((DEPTH, CONV_WIDTH, D_MODEL), CONV_WIDTH ** -0.5),
        "w_p_x": nrm((DEPTH, X_WIDTH, D_MODEL), X_WIDTH ** -0.5),
        "b_merge": nrm((DEPTH, N_BRANCH * D_MODEL), 0.02),
        "w_out": nrm((DEPTH, D_MODEL, D_MODEL), D_MODEL ** -0.5),
    }


def reference(x_prompt, x_sample, mem_prompt, cache_ckv, cache_krope, cache_mem_k, cache_mem_v, state_conv,
              g_pre, w_in, g_qlat, w_uq, g_kvlat, w_ukv, g_qnorm, g_knorm, conv_w, conv_b, ln_g, ln_b,
              w_pw2, b_pw2, g_mem, w_mem_kv, g_xq, g_xk, w_p_mla, w_p_conv, w_p_x, b_merge, w_out):
    xp, xs = x_prompt, x_sample
    p_ckv, p_kr, p_mk, p_mv, p_cv = [], [], [], [], []
    s_ckv, s_kr, s_cv = [], [], []
    for l in range(DEPTH):
        lp = dict(g_pre=g_pre[l], w_in=w_in[l], g_qlat=g_qlat[l], w_uq=w_uq[l], g_kvlat=g_kvlat[l],
                  w_ukv=w_ukv[l], g_qnorm=g_qnorm[l], g_knorm=g_knorm[l], conv_w=conv_w[l], conv_b=conv_b[l],
                  ln_g=ln_g[l], ln_b=ln_b[l], w_pw2=w_pw2[l], b_pw2=b_pw2[l], g_mem=g_mem[l],
                  w_mem_kv=w_mem_kv[l], g_xq=g_xq[l], g_xk=g_xk[l], w_p_mla=w_p_mla[l], w_p_conv=w_p_conv[l],
                  w_p_x=w_p_x[l], b_merge=b_merge[l], w_out=w_out[l])
        xp, ckv, kr, mk, mv, cv = prompt_layer(xp, mem_prompt, lp)
        p_ckv.append(ckv); p_kr.append(kr); p_mk.append(mk); p_mv.append(mv); p_cv.append(cv)
        xs, ckv, kr, cv = sample_layer(xs, cache_ckv[l], cache_krope[l], cache_mem_k[l], cache_mem_v[l],
                                       state_conv[l], lp)
        s_ckv.append(ckv); s_kr.append(kr); s_cv.append(cv)
    return (xp, xs, jnp.stack(p_ckv), jnp.stack(p_kr), jnp.stack(p_mk), jnp.stack(p_mv), jnp.stack(p_cv),
            jnp.stack(s_ckv), jnp.stack(s_kr), jnp.stack(s_cv))
```

```python
import functools
import math

import jax
import jax.numpy as jnp
from jax import lax
from jax.experimental import pallas as pl
from jax.experimental.pallas import tpu as pltpu

F32 = jnp.float32
BF16 = jnp.bfloat16

EPS = 1e-6
CHUNK = 64
ROPE_THETA = 10000.0
NEG_BIG = -1e30
V7X_VMEM_LIMIT = 56 * 1024 * 1024
LANE = 128
SUBLANE = 8
HALO = 32

ROW_TILE = 1024
COL_TILE = 1024
NORM_TILE = 512
MERGE_TILE = 512
PROMPT_ATTN_TILE = 1024
SAMPLE_KEY_TILE = 512


def _cparams(*sem):
    return pltpu.CompilerParams(dimension_semantics=sem, vmem_limit_bytes=V7X_VMEM_LIMIT)


def _tile(n, pref):
    t = min(n, pref)
    while n % t:
        t //= 2
    return t


def _sigmoid(x):
    return 1.0 / (1.0 + jnp.exp(-x))


def _silu(x):
    return x * _sigmoid(x)


def _rms(x, g):
    return x * lax.rsqrt(jnp.mean(x * x, axis=-1, keepdims=True) + EPS) * g


def _rope_rows(x, cos2, sin2):
    half = x.shape[-1] // 2
    swapped = jnp.concatenate([x[:, half:], x[:, :half]], axis=-1)
    return x * cos2 + swapped * sin2


def _rms_cast_kernel(x_ref, g_ref, o_ref):
    o_ref[...] = _rms(x_ref[...], g_ref[...]).astype(o_ref.dtype)


def _rms_cast(x, g):
    m, d = x.shape
    tm = _tile(m, NORM_TILE)
    return pl.pallas_call(
        _rms_cast_kernel,
        grid=(m // tm,),
        in_specs=[pl.BlockSpec((tm, d), lambda i: (i, 0)), pl.BlockSpec((1, d), lambda i: (0, 0))],
        out_specs=pl.BlockSpec((tm, d), lambda i: (i, 0)),
        out_shape=jax.ShapeDtypeStruct((m, d), BF16),
        compiler_params=_cparams("parallel"),
        name="rms_cast",
    )(x, g.reshape(1, d))


def _mm_kernel(a_ref, w_ref, *rest, has_res):
    o_ref = rest[-1]
    acc = jnp.dot(a_ref[...], w_ref[...], preferred_element_type=F32)
    if has_res:
        acc = rest[0][...] + acc
    o_ref[...] = acc.astype(o_ref.dtype)


def _matmul(a, w, res=None, out_dtype=F32, name="matmul"):
    m, k = a.shape
    n = w.shape[1]
    tm, tn = _tile(m, ROW_TILE), _tile(n, COL_TILE)
    in_specs = [pl.BlockSpec((tm, k), lambda i, j: (i, 0)), pl.BlockSpec((k, tn), lambda i, j: (0, j))]
    args = [a, w]
    if res is not None:
        in_specs.append(pl.BlockSpec((tm, tn), lambda i, j: (i, j)))
        args.append(res)
    return pl.pallas_call(
        functools.partial(_mm_kernel, has_res=res is not None),
        grid=(m // tm, n // tn),
        in_specs=in_specs,
        out_specs=pl.BlockSpec((tm, tn), lambda i, j: (i, j)),
        out_shape=jax.ShapeDtypeStruct((m, n), out_dtype),
        compiler_params=_cparams("parallel", "parallel"),
        name=name,
    )(*args)


def _q_kernel(ql_ref, gl_ref, w_ref, gq_ref, cos_ref, sin_ref, o_ref, c_scr, *, nope, scale):
    @pl.when(pl.program_id(1) == 0)
    def _():
        c_scr[...] = _rms(ql_ref[...], gl_ref[...]).astype(BF16)

    q = jnp.dot(c_scr[...], w_ref[0], preferred_element_type=F32)
    qn = _rms(q, gq_ref[...])
    o_ref[0, :, :nope] = (qn[:, :nope] * scale).astype(o_ref.dtype)
    o_ref[0, :, nope:] = (_rope_rows(qn[:, nope:], cos_ref[...], sin_ref[...]) * scale).astype(o_ref.dtype)


def _q_proj(proj, seg, g_lat, w_r, g_qn, cos2, sin2, nope, scale):
    t = proj.shape[0]
    off, width = seg
    heads, _, qk = w_r.shape
    rope = qk - nope
    tm = _tile(t, ROW_TILE)
    assert off % width == 0
    return pl.pallas_call(
        functools.partial(_q_kernel, nope=nope, scale=scale),
        grid=(t // tm, heads),
        in_specs=[
            pl.BlockSpec((tm, width), lambda i, h: (i, off // width)),
            pl.BlockSpec((1, width), lambda i, h: (0, 0)),
            pl.BlockSpec((1, width, qk), lambda i, h: (h, 0, 0)),
            pl.BlockSpec((1, qk), lambda i, h: (0, 0)),
            pl.BlockSpec((tm, rope), lambda i, h: (i, 0)),
            pl.BlockSpec((tm, rope), lambda i, h: (i, 0)),
        ],
        out_specs=pl.BlockSpec((1, tm, qk), lambda i, h: (h, i, 0)),
        out_shape=jax.ShapeDtypeStruct((heads, t, qk), BF16),
        scratch_shapes=[pltpu.VMEM((tm, width), BF16)],
        compiler_params=_cparams("parallel", "arbitrary"),
        name="q_proj",
    )(proj, g_lat.reshape(1, width), w_r, g_qn.reshape(1, qk), cos2, sin2)


def _kv_kernel(c_ref, gl_ref, w_ref, kr_ref, gk_ref, cos_ref, sin_ref, k_ref, v_ref, c_scr, *, nope, rope):
    @pl.when(pl.program_id(1) == 0)
    def _():
        c_scr[...] = _rms(c_ref[...], gl_ref[...]).astype(BF16)

    kv = jnp.dot(c_scr[...], w_ref[0], preferred_element_type=F32)
    kn = kv[:, :nope]
    kr = kr_ref[:, :rope]
    ssq = jnp.sum(kn * kn, axis=-1, keepdims=True) + jnp.sum(kr * kr, axis=-1, keepdims=True)
    r = lax.rsqrt(ssq / (nope + rope) + EPS)
    g = gk_ref[...]
    k_ref[0, :, :nope] = (kn * r * g[:, :nope]).astype(k_ref.dtype)
    k_ref[0, :, nope:] = _rope_rows(kr * r * g[:, nope:], cos_ref[...], sin_ref[...]).astype(k_ref.dtype)
    v_ref[0] = kv[:, nope:].astype(v_ref.dtype)


def _kv_proj(proj, seg, krp, rows, g_lat, w_r, g_kn, cos2, sin2, nope, rope):
    off, width = seg
    heads, _, nv = w_r.shape
    vh = nv - nope
    tm = _tile(rows, ROW_TILE)
    assert off % width == 0
    return pl.pallas_call(
        functools.partial(_kv_kernel, nope=nope, rope=rope),
        grid=(rows // tm, heads),
        in_specs=[
            pl.BlockSpec((tm, width), lambda i, h: (i, off // width)),
            pl.BlockSpec((1, width), lambda i, h: (0, 0)),
            pl.BlockSpec((1, width, nv), lambda i, h: (h, 0, 0)),
            pl.BlockSpec((tm, krp.shape[1]), lambda i, h: (i, 0)),
            pl.BlockSpec((1, nope + rope), lambda i, h: (0, 0)),
            pl.BlockSpec((tm, rope), lambda i, h: (i, 0)),
            pl.BlockSpec((tm, rope), lambda i, h: (i, 0)),
        ],
        out_specs=[
            pl.BlockSpec((1, tm, nope + rope), lambda i, h: (h, i, 0)),
            pl.BlockSpec((1, tm, vh), lambda i, h: (h, i, 0)),
        ],
        out_shape=[
            jax.ShapeDtypeStruct((heads, rows, nope + rope), BF16),
            jax.ShapeDtypeStruct((heads, rows, vh), BF16),
        ],
        scratch_shapes=[pltpu.VMEM((tm, width), BF16)],
        compiler_params=_cparams("parallel", "arbitrary"),
        name="kv_proj",
    )(proj, g_lat.reshape(1, width), w_r, krp, g_kn.reshape(1, nope + rope), cos2, sin2)


def _softmax_step(s, v, m_scr, l_scr, acc_scr):
    m_prev = m_scr[...]
    m_new = jnp.maximum(m_prev, jnp.max(s, axis=-1, keepdims=True))
    alpha = jnp.exp(m_prev - m_new)
    p = jnp.exp(s - m_new)
    l_scr[...] = alpha * l_scr[...] + jnp.sum(p, axis=-1, keepdims=True)
    acc_scr[...] = alpha * acc_scr[...] + jnp.dot(p.astype(BF16), v, preferred_element_type=F32)
    m_scr[...] = m_new


def _pattn_kernel(qi_tab, ki_tab, q_ref, k_ref, v_ref, z_ref, o_ref, m_scr, l_scr, acc_scr, *, chunk):
    p_id = pl.program_id(1)
    qi = qi_tab[p_id]
    ki = ki_tab[p_id]

    @pl.when(ki == 0)
    def _():
        m_scr[...] = jnp.full_like(m_scr, NEG_BIG)
        l_scr[...] = jnp.zeros_like(l_scr)
        acc_scr[...] = jnp.zeros_like(acc_scr)

    s = lax.dot_general(q_ref[0], k_ref[0], (((1,), (1,)), ((), ())), preferred_element_type=F32)

    @pl.when(ki < qi)
    def _():
        _softmax_step(s, v_ref[0], m_scr, l_scr, acc_scr)

    @pl.when(ki == qi)
    def _():
        row_chunk = lax.broadcasted_iota(jnp.int32, s.shape, 0) // chunk
        col_chunk = lax.broadcasted_iota(jnp.int32, s.shape, 1) // chunk
        _softmax_step(jnp.where(col_chunk <= row_chunk, s, NEG_BIG), v_ref[0], m_scr, l_scr, acc_scr)
        o = acc_scr[...] / l_scr[...]
        o_ref[...] = (o * _silu(z_ref[...])).astype(o_ref.dtype)


def _prompt_attention(q, k, v, proj, z_seg, rows):
    heads, _, qk = q.shape
    vh = v.shape[2]
    z_off, _ = z_seg
    tq = _tile(rows, PROMPT_ATTN_TILE)
    assert tq % CHUNK == 0 and z_off % vh == 0
    nq = rows // tq
    pairs = [(a, b) for a in range(nq) for b in range(a + 1)]
    qi_tab = jnp.array([a for a, _ in pairs], jnp.int32)
    ki_tab = jnp.array([b for _, b in pairs], jnp.int32)
    grid_spec = pltpu.PrefetchScalarGridSpec(
        num_scalar_prefetch=2,
        grid=(heads, len(pairs)),
        in_specs=[
            pl.BlockSpec((1, tq, qk), lambda h, p, qt, kt: (h, qt[p], 0)),
            pl.BlockSpec((1, tq, qk), lambda h, p, qt, kt: (h, kt[p], 0)),
            pl.BlockSpec((1, tq, vh), lambda h, p, qt, kt: (h, kt[p], 0)),
            pl.BlockSpec((tq, vh), lambda h, p, qt, kt: (qt[p], z_off // vh + h)),
        ],
        out_specs=pl.BlockSpec((tq, vh), lambda h, p, qt, kt: (qt[p], h)),
        scratch_shapes=[pltpu.VMEM((tq, 1), F32), pltpu.VMEM((tq, 1), F32), pltpu.VMEM((tq, vh), F32)],
    )
    return pl.pallas_call(
        functools.partial(_pattn_kernel, chunk=CHUNK),
        grid_spec=grid_spec,
        out_shape=jax.ShapeDtypeStruct((rows, heads * vh), BF16),
        compiler_params=_cparams("parallel", "arbitrary"),
        name="prompt_attention",
    )(qi_tab, ki_tab, q, k, v, proj)


def _sattn_kernel(q_ref, cp_ref, cn_ref, krp_ref, krn_ref, cos_ref, sin_ref, gl_ref, gn_ref, gr_ref,
                  wuk_ref, wuv_ref, z_ref, o_ref, qa_scr, qr_scr, m_scr, l_scr, acc_scr,
                  *, heads, nope, rope, vh, n_past_tiles, ls):
    kt = pl.program_id(1)
    qk = nope + rope
    half = rope // 2

    @pl.when(kt == 0)
    def _():
        m_scr[...] = jnp.full_like(m_scr, NEG_BIG)
        l_scr[...] = jnp.zeros_like(l_scr)
        acc_scr[...] = jnp.zeros_like(acc_scr)
        for h in range(heads):
            qh = q_ref[h]
            qn = (qh[:, :nope].astype(F32) * gn_ref[...]).astype(BF16)
            qa = jnp.dot(qn, wuk_ref[h * nope:(h + 1) * nope, :], preferred_element_type=F32)
            qa_scr[h * ls:(h + 1) * ls, :] = qa.astype(BF16)
            qr_scr[h * ls:(h + 1) * ls, :] = qh[:, nope:]

    def step(c, kr, cos_t, sin_t):
        n = c.shape[0]
        cn = _rms(c, gl_ref[...]).astype(BF16)
        kn_t = lax.dot_general(wuk_ref[...], cn, (((1,), (1,)), ((), ())), preferred_element_type=F32)
        ssq = jnp.sum((kn_t * kn_t).reshape(heads, nope, n), axis=1)
        kr_t = kr.T
        ssq_r = jnp.sum(kr_t * kr_t, axis=0, keepdims=True)
        rk = lax.rsqrt((ssq + ssq_r) / qk + EPS)
        krg = kr_t * gr_ref[...]
        x1, x2 = krg[:half], krg[half:]
        kr_rot = jnp.concatenate([x1 * cos_t - x2 * sin_t, x2 * cos_t + x1 * sin_t], axis=0).astype(BF16)
        s = lax.dot_general(qa_scr[...], cn, (((1,), (1,)), ((), ())), preferred_element_type=F32)
        s = s + jnp.dot(qr_scr[...], kr_rot, preferred_element_type=F32)
        s = (s.reshape(heads, ls, n) * rk[:, None, :]).reshape(heads * ls, n)
        _softmax_step(s, cn, m_scr, l_scr, acc_scr)

    @pl.when(kt < n_past_tiles)
    def _():
        step(cp_ref[0], krp_ref[0], cos_ref[...], sin_ref[...])

    @pl.when(kt == n_past_tiles)
    def _():
        step(cn_ref[...], krn_ref[:, :rope], cos_ref[:, :ls], sin_ref[:, :ls])
        ol = (acc_scr[...] / l_scr[...]).astype(BF16)
        for h in range(heads):
            oh = jnp.dot(ol[h * ls:(h + 1) * ls, :], wuv_ref[h], preferred_element_type=F32)
            zh = z_ref[:, h * vh:(h + 1) * vh]
            o_ref[:, h * vh:(h + 1) * vh] = (oh * _silu(zh)).astype(o_ref.dtype)


def _sample_attention(q, cache_ckv, cache_kr, proj, c_seg, z_seg, krp, row0, cos_t, sin_t, g_lat, g_kn,
                      wuk_t, wuv, nope):
    heads, _, qk = q.shape
    bs, past, kl = cache_ckv.shape
    rope = qk - nope
    vh = wuv.shape[2]
    ls = (proj.shape[0] - row0) // bs
    tk = _tile(past, SAMPLE_KEY_TILE)
    npt = past // tk
    c_off, c_w = c_seg
    z_off, z_w = z_seg
    assert c_off % c_w == 0 and z_off % z_w == 0 and row0 % ls == 0 and ls <= tk
    rb0 = row0 // ls
    last = lambda kt: jnp.minimum(kt, npt - 1)
    kern = functools.partial(_sattn_kernel, heads=heads, nope=nope, rope=rope, vh=vh, n_past_tiles=npt, ls=ls)
    return pl.pallas_call(
        kern,
        grid=(bs, npt + 1),
        in_specs=[
            pl.BlockSpec((heads, ls, qk), lambda b, kt: (0, rb0 + b, 0)),
            pl.BlockSpec((1, tk, kl), lambda b, kt: (b, last(kt), 0)),
            pl.BlockSpec((ls, c_w), lambda b, kt: (rb0 + b, c_off // c_w)),
            pl.BlockSpec((1, tk, rope), lambda b, kt: (b, last(kt), 0)),
            pl.BlockSpec((ls, krp.shape[1]), lambda b, kt: (rb0 + b, 0)),
            pl.BlockSpec((rope // 2, tk), lambda b, kt: (0, kt)),
            pl.BlockSpec((rope // 2, tk), lambda b, kt: (0, kt)),
            pl.BlockSpec((1, kl), lambda b, kt: (0, 0)),
            pl.BlockSpec((1, nope), lambda b, kt: (0, 0)),
            pl.BlockSpec((rope, 1), lambda b, kt: (0, 0)),
            pl.BlockSpec((heads * nope, kl), lambda b, kt: (0, 0)),
            pl.BlockSpec((heads, kl, vh), lambda b, kt: (0, 0, 0)),
            pl.BlockSpec((ls, z_w), lambda b, kt: (rb0 + b, z_off // z_w)),
        ],
        out_specs=pl.BlockSpec((ls, heads * vh), lambda b, kt: (b, 0)),
        out_shape=jax.ShapeDtypeStruct((bs * ls, heads * vh), BF16),
        scratch_shapes=[
            pltpu.VMEM((heads * ls, kl), BF16),
            pltpu.VMEM((heads * ls, rope), BF16),
            pltpu.VMEM((heads * ls, 1), F32),
            pltpu.VMEM((heads * ls, 1), F32),
            pltpu.VMEM((heads * ls, kl), F32),
        ],
        compiler_params=_cparams("parallel", "arbitrary"),
        name="sample_attention",
    )(q, cache_ckv, proj, cache_kr, krp, cos_t, sin_t, g_lat.reshape(1, kl), g_kn[:nope].reshape(1, nope),
      g_kn[nope:].reshape(rope, 1), wuk_t, wuv, proj)


def _conv_kernel(u2_ref, halo_ref, cw_ref, cb_ref, lg_ref, lb_ref, wpw_ref, bpw_ref, z_ref,
                 u_ref, a_ref, p_scr, y_scr, *, nseq, seq_len, width, taps, halo_raw, rc, lc):
    u2 = u2_ref[...]
    u = u2[:, :width] * _sigmoid(u2[:, width:])
    u_ref[...] = u
    for s in range(nseq):
        if halo_raw:
            hr = halo_ref[...]
            hu = hr[:, :width] * _sigmoid(hr[:, width:])
            hu = jnp.where(pl.program_id(0) == 0, 0.0, hu)
        else:
            hu = halo_ref[s]
        p_scr[s, 0:HALO, :] = hu
        p_scr[s, HALO:HALO + seq_len, :] = u[s * seq_len:(s + 1) * seq_len]

    shift = HALO - (taps - 1)
    for s in range(nseq):
        def rows(r, carry, s=s):
            base = pl.multiple_of(r * rc, rc)
            for c0 in range(0, width, lc):
                win = p_scr[s, pl.ds(base, rc + HALO), c0:c0 + lc]
                acc = jnp.zeros((rc, lc), F32)
                for sub in range(SUBLANE):
                    a_max = (HALO - sub) // SUBLANE * SUBLANE
                    ws = win[sub:sub + rc + a_max]
                    for a in range(0, a_max + 1, SUBLANE):
                        k = a + sub - shift
                        if 0 <= k < taps:
                            acc = acc + ws[a:a + rc] * cw_ref[k:k + 1, c0:c0 + lc]
                y_scr[pl.ds(s * seq_len + base, rc), c0:c0 + lc] = acc + cb_ref[:, c0:c0 + lc]
            return carry
        lax.fori_loop(0, seq_len // rc, rows, 0)

    y = y_scr[...]
    mu = jnp.mean(y, axis=-1, keepdims=True)
    yc = y - mu
    var = jnp.mean(yc * yc, axis=-1, keepdims=True)
    yn = yc * lax.rsqrt(var + EPS) * lg_ref[...] + lb_ref[...]
    o = jnp.dot(_silu(yn).astype(BF16), wpw_ref[...], preferred_element_type=F32) + bpw_ref[...]
    a_ref[...] = (o * _silu(z_ref[...])).astype(a_ref.dtype)


def _conv_branch(proj, u_seg, z_seg, row0, nrows, seq_len, state, cw, cb, lg, lb, wpw, bpw):
    width = cw.shape[1]
    taps = cw.shape[0]
    u_off, u_w = u_seg
    z_off, z_w = z_seg
    assert u_off % u_w == 0 and z_off % z_w == 0 and taps - 1 <= HALO
    if state is None:
        tm = _tile(nrows, NORM_TILE)
        nseq, sl = 1, tm
        halo_spec = pl.BlockSpec((HALO, u_w), lambda i: (jnp.maximum((row0 + i * tm) // HALO - 1, 0), u_off // u_w))
        halo_arg = proj
    else:
        nseq = _tile(nrows // seq_len, max(NORM_TILE // seq_len, 1))
        sl = seq_len
        tm = nseq * sl
        halo_spec = pl.BlockSpec((nseq, HALO, width), lambda i: (i, 0, 0))
        halo_arg = state
    assert row0 % tm == 0 and tm % HALO == 0
    rb0 = row0 // tm
    rc = _tile(sl, 64)
    lc = _tile(width, LANE)
    vec = lambda a: a.reshape(1, width)
    kern = functools.partial(_conv_kernel, nseq=nseq, seq_len=sl, width=width, taps=taps,
                             halo_raw=state is None, rc=rc, lc=lc)
    return pl.pallas_call(
        kern,
        grid=(nrows // tm,),
        in_specs=[
            pl.BlockSpec((tm, u_w), lambda i: (rb0 + i, u_off // u_w)),
            halo_spec,
            pl.BlockSpec((taps, width), lambda i: (0, 0)),
            pl.BlockSpec((1, width), lambda i: (0, 0)),
            pl.BlockSpec((1, width), lambda i: (0, 0)),
            pl.BlockSpec((1, width), lambda i: (0, 0)),
            pl.BlockSpec((width, width), lambda i: (0, 0)),
            pl.BlockSpec((1, width), lambda i: (0, 0)),
            pl.BlockSpec((tm, z_w), lambda i: (rb0 + i, z_off // z_w)),
        ],
        out_specs=[pl.BlockSpec((tm, width), lambda i: (i, 0)), pl.BlockSpec((tm, width), lambda i: (i, 0))],
        out_shape=[jax.ShapeDtypeStruct((nrows, width), F32), jax.ShapeDtypeStruct((nrows, width), BF16)],
        scratch_shapes=[pltpu.VMEM((nseq, HALO + sl, width), F32), pltpu.VMEM((tm, width), F32)],
        compiler_params=_cparams("arbitrary"),
        name="conv_branch",
    )(proj, halo_arg, cw, vec(cb), vec(lg), vec(lb), wpw, vec(bpw), proj)


def _memkv_kernel(m_ref, g_ref, w_ref, gk_ref, k_ref, v_ref, *, xw, xh):
    h = _rms(m_ref[...], g_ref[...]).astype(BF16)
    kv = jnp.dot(h, w_ref[...], preferred_element_type=F32)
    for i in range(xw // xh):
        k_ref[:, i * xh:(i + 1) * xh] = _rms(kv[:, i * xh:(i + 1) * xh], gk_ref[...])
    v_ref[...] = kv[:, xw:]


def _memory_kv(mem, g_mem, w_kv, g_xk):
    n, d = mem.shape
    xh = g_xk.shape[0]
    xw = w_kv.shape[1] // 2
    full = lambda shape: pl.BlockSpec(shape, lambda i: tuple(0 for _ in shape))
    return pl.pallas_call(
        functools.partial(_memkv_kernel, xw=xw, xh=xh),
        grid=(1,),
        in_specs=[full((n, d)), full((1, d)), full((d, 2 * xw)), full((1, xh))],
        out_specs=[full((n, xw)), full((n, xw))],
        out_shape=[jax.ShapeDtypeStruct((n, xw), F32), jax.ShapeDtypeStruct((n, xw), F32)],
        compiler_params=_cparams("arbitrary"),
        name="memory_kv",
    )(mem, g_mem.reshape(1, d), w_kv, g_xk.reshape(1, xh))


def _xattn_kernel(xq_ref, z_ref, g_ref, mk_ref, mv_ref, a_ref, *, xheads, xh, scale):
    for h in range(xheads):
        sl = slice(h * xh, (h + 1) * xh)
        q = (_rms(xq_ref[:, sl], g_ref[...]) * scale).astype(BF16)
        s = lax.dot_general(q, mk_ref[0, :, sl].astype(BF16), (((1,), (1,)), ((), ())),
                            preferred_element_type=F32)
        e = jnp.exp(s - jnp.max(s, axis=-1, keepdims=True))
        p = e / jnp.sum(e, axis=-1, keepdims=True)
        o = jnp.dot(p.astype(BF16), mv_ref[0, :, sl].astype(BF16), preferred_element_type=F32)
        a_ref[:, sl] = (o * _silu(z_ref[:, sl])).astype(a_ref.dtype)


def _cross_attention(proj, q_seg, z_seg, row0, nrows, tm, mk, mv, per_tile_kv, g_xq, xh):
    q_off, xw = q_seg
    z_off, _ = z_seg
    assert q_off % xw == 0 and z_off % xw == 0 and row0 % tm == 0
    rb0 = row0 // tm
    nm = mk.shape[1]
    kv_map = (lambda i: (i, 0, 0)) if per_tile_kv else (lambda i: (0, 0, 0))
    kern = functools.partial(_xattn_kernel, xheads=xw // xh, xh=xh, scale=xh ** -0.5)
    return pl.pallas_call(
        kern,
        grid=(nrows // tm,),
        in_specs=[
            pl.BlockSpec((tm, xw), lambda i: (rb0 + i, q_off // xw)),
            pl.BlockSpec((tm, xw), lambda i: (rb0 + i, z_off // xw)),
            pl.BlockSpec((1, xh), lambda i: (0, 0)),
            pl.BlockSpec((1, nm, xw), kv_map),
            pl.BlockSpec((1, nm, xw), kv_map),
        ],
        out_specs=pl.BlockSpec((tm, xw), lambda i: (i, 0)),
        out_shape=jax.ShapeDtypeStruct((nrows, xw), BF16),
        compiler_params=_cparams("parallel"),
        name="cross_attention",
    )(proj, proj, g_xq.reshape(1, xh), mk, mv)


def _merge_kernel(am_ref, ac_ref, ax_ref, wm_ref, wc_ref, wx_ref, g0_ref, g1_ref, g2_ref,
                  b0_ref, b1_ref, b2_ref, o_ref):
    pm = jnp.dot(am_ref[...], wm_ref[...], preferred_element_type=F32)
    pc = jnp.dot(ac_ref[...], wc_ref[...], preferred_element_type=F32)
    px = jnp.dot(ax_ref[...], wx_ref[...], preferred_element_type=F32)
    merged = (_sigmoid(g0_ref[...] + b0_ref[...]) * pm + _sigmoid(g1_ref[...] + b1_ref[...]) * pc
              + _sigmoid(g2_ref[...] + b2_ref[...]) * px)
    o_ref[...] = merged.astype(o_ref.dtype)


def _merge(a_mla, a_conv, a_x, w_m, w_c, w_x, proj, g_seg, b_merge):
    t = a_mla.shape[0]
    d = w_m.shape[1]
    g_off, _ = g_seg
    tm, tn = _tile(t, MERGE_TILE), _tile(d, MERGE_TILE)
    assert g_off % tn == 0 and d % tn == 0
    gate = lambda g: pl.BlockSpec((tm, tn), lambda i, j: (i, (g_off + g * d) // tn + j))
    bias = lambda g: pl.BlockSpec((1, tn), lambda i, j: (0, g * d // tn + j))
    rows = lambda a: pl.BlockSpec((tm, a.shape[1]), lambda i, j: (i, 0))
    cols = lambda w: pl.BlockSpec((w.shape[0], tn), lambda i, j: (0, j))
    b2d = b_merge.reshape(1, -1)
    return pl.pallas_call(
        _merge_kernel,
        grid=(t // tm, d // tn),
        in_specs=[rows(a_mla), rows(a_conv), rows(a_x), cols(w_m), cols(w_c), cols(w_x),
                  gate(0), gate(1), gate(2), bias(0), bias(1), bias(2)],
        out_specs=pl.BlockSpec((tm, tn), lambda i, j: (i, j)),
        out_shape=jax.ShapeDtypeStruct((t, d), BF16),
        compiler_params=_cparams("parallel", "parallel"),
        name="merge",
    )(a_mla, a_conv, a_x, w_m, w_c, w_x, proj, proj, proj, b2d, b2d, b2d)


def kernel(x_prompt, x_sample, mem_prompt, cache_ckv, cache_krope, cache_mem_k, cache_mem_v, state_conv,
           g_pre, w_in, g_qlat, w_uq, g_kvlat, w_ukv, g_qnorm, g_knorm, conv_w, conv_b, ln_g, ln_b,
           w_pw2, b_pw2, g_mem, w_mem_kv, g_xq, g_xk, w_p_mla, w_p_conv, w_p_x, b_merge, w_out):
    depth = w_in.shape[0]
    bp, sp, d = x_prompt.shape
    bs, ls, _ = x_sample.shape
    past = cache_ckv.shape[2]
    ql, kl = g_qlat.shape[1], g_kvlat.shape[1]
    qk, rope = g_qnorm.shape[1], cache_krope.shape[3]
    nope = qk - rope
    heads = w_uq.shape[2] // qk
    vh = w_ukv.shape[2] // heads - nope
    cw = conv_w.shape[2]
    taps = conv_w.shape[1]
    n_mem, xheads, xh = cache_mem_k.shape[2:]
    xw = xheads * xh
    assert bp == 1, "the prompt group is one sequence"
    assert min(sp, ls) >= taps - 1, "the carried conv state must come from the new rows alone"
    tp = bp * sp
    ts = bs * ls
    half = rope // 2

    in_sizes = dict(q_lat=ql, c_kv=kl, k_rope=rope, z_mla=heads * vh, u2=2 * cw, z_conv=cw, xq=xw, z_x=xw, gate=3 * d)
    src, o = {}, 0
    for name in ("q_lat", "c_kv", "k_rope", "z_mla", "u2", "z_conv", "xq", "z_x", "gate"):
        src[name] = o
        o += in_sizes[name]
    order = ("gate", "z_mla", "u2", "z_conv", "xq", "z_x", "q_lat", "c_kv")
    seg, o = {}, 0
    for name in order:
        seg[name] = (o, in_sizes[name])
        o += in_sizes[name]
    gate_seg = (seg["gate"][0], d)
    zm_head_seg = (seg["z_mla"][0], vh)

    inv = jnp.power(ROPE_THETA, -jnp.arange(half, dtype=F32) / half)
    pos_rows = jnp.concatenate([jnp.arange(tp), past + jnp.tile(jnp.arange(ls), bs)]).astype(F32)
    ang = pos_rows[:, None] * inv[None, :]
    cos2 = jnp.concatenate([jnp.cos(ang), jnp.cos(ang)], axis=1)
    sin2 = jnp.concatenate([-jnp.sin(ang), jnp.sin(ang)], axis=1)
    tk_s = _tile(past, SAMPLE_KEY_TILE)
    kpos = jnp.arange(past + tk_s).astype(F32)
    ang_t = inv[:, None] * kpos[None, :]
    cos_t, sin_t = jnp.cos(ang_t), jnp.sin(ang_t)

    x = jnp.concatenate([x_prompt.reshape(tp, d), x_sample.reshape(ts, d)], axis=0)
    mem = mem_prompt.reshape(n_mem, d)
    outs = {k: [] for k in ("p_ckv", "p_kr", "p_mk", "p_mv", "p_cv", "s_ckv", "s_kr", "s_cv")}

    for l in range(depth):
        wl = w_in[l]
        w_main = jnp.concatenate([wl[:, src[n]:src[n] + in_sizes[n]] for n in order], axis=1).astype(BF16)
        w_kr = jnp.pad(wl[:, src["k_rope"]:src["k_rope"] + rope], ((0, 0), (0, LANE - rope))).astype(BF16)
        w_uq_r = w_uq[l].reshape(ql, heads, qk).transpose(1, 0, 2).astype(BF16)
        w_ukv4 = w_ukv[l].reshape(kl, heads, nope + vh)
        w_ukv_r = w_ukv4.transpose(1, 0, 2).astype(BF16)
        wuk_t = w_ukv4[:, :, :nope].transpose(1, 2, 0).reshape(heads * nope, kl).astype(BF16)
        wuv = w_ukv4[:, :, nope:].transpose(1, 0, 2).astype(BF16)

        h = _rms_cast(x, g_pre[l])
        proj = _matmul(h, w_main, name="in_proj")
        krp = _matmul(h, w_kr, name="rope_key_proj")

        q = _q_proj(proj, seg["q_lat"], g_qlat[l], w_uq_r, g_qnorm[l], cos2, sin2, nope, qk ** -0.5)
        k_p, v_p = _kv_proj(proj, seg["c_kv"], krp, tp, g_kvlat[l], w_ukv_r, g_knorm[l], cos2, sin2, nope, rope)
        a_mla_p = _prompt_attention(q, k_p, v_p, proj, zm_head_seg, tp)
        a_mla_s = _sample_attention(q, cache_ckv[l], cache_krope[l], proj, seg["c_kv"], seg["z_mla"], krp, tp,
                                    cos_t, sin_t, g_kvlat[l], g_knorm[l], wuk_t, wuv, nope)

        conv_args = (conv_w[l], conv_b[l], ln_g[l], ln_b[l], w_pw2[l].astype(BF16), b_pw2[l])
        u_p, a_conv_p = _conv_branch(proj, seg["u2"], seg["z_conv"], 0, tp, sp, None, *conv_args)
        state = jnp.pad(state_conv[l], ((0, 0), (HALO - (taps - 1), 0), (0, 0)))
        u_s, a_conv_s = _conv_branch(proj, seg["u2"], seg["z_conv"], tp, ts, ls, state, *conv_args)

        mk, mv = _memory_kv(mem, g_mem[l], w_mem_kv[l].astype(BF16), g_xk[l])
        a_x_p = _cross_attention(proj, seg["xq"], seg["z_x"], 0, tp, _tile(tp, NORM_TILE), mk[None], mv[None], False,
                                 g_xq[l], xh)
        a_x_s = _cross_attention(proj, seg["xq"], seg["z_x"], tp, ts, ls, cache_mem_k[l].reshape(bs, n_mem, xw),
                                 cache_mem_v[l].reshape(bs, n_mem, xw), True, g_xq[l], xh)

        merged = _merge(jnp.concatenate([a_mla_p, a_mla_s]), jnp.concatenate([a_conv_p, a_conv_s]),
                        jnp.concatenate([a_x_p, a_x_s]), w_p_mla[l].astype(BF16), w_p_conv[l].astype(BF16),
                        w_p_x[l].astype(BF16), proj, gate_seg, b_merge[l])
        x = _matmul(merged, w_out[l].astype(BF16), res=x, name="out_proj")

        c0, k0 = seg["c_kv"][0], taps - 1
        outs["p_ckv"].append(proj[:tp, c0:c0 + kl].reshape(bp, sp, kl))
        outs["s_ckv"].append(proj[tp:, c0:c0 + kl].reshape(bs, ls, kl))
        outs["p_kr"].append(krp[:tp, :rope].reshape(bp, sp, rope))
        outs["s_kr"].append(krp[tp:, :rope].reshape(bs, ls, rope))
        outs["p_mk"].append(mk.reshape(bp, n_mem, xheads, xh))
        outs["p_mv"].append(mv.reshape(bp, n_mem, xheads, xh))
        outs["p_cv"].append(u_p[tp - k0:].reshape(bp, k0, cw))
        outs["s_cv"].append(u_s.reshape(bs, ls, cw)[:, ls - k0:])

    st = lambda name: jnp.stack(outs[name])
    return (x[:tp].reshape(bp, sp, d), x[tp:].reshape(bs, ls, d), st("p_ckv"), st("p_kr"), st("p_mk"), st("p_mv"),
            st("p_cv"), st("s_ckv"), st("s_kr"), st("s_cv"))
```
